```python
import math
import jax, jax.numpy as jnp
from jax import lax
import numpy as np

D_MODEL = 4096
BATCH = 2
SEQ = 8192
DEPTH = 1
DEC_BATCH = 32
DEC_SEQ = 64
PAST_LEN = 2048

CHUNK = 64
D_MIX = D_MODEL
GDN_DK = 128
GDN_DV = 128
GDN_HEADS = (D_MIX // 2) // GDN_DV
GDN_QK = GDN_HEADS * GDN_DK
GDN_V = GDN_HEADS * GDN_DV
ML_HEADS = 8
ML_DV = (D_MIX // 2) // ML_HEADS
ML_DQK = ML_DV // 2
ML_QK = ML_HEADS * ML_DQK
ML_V = ML_HEADS * ML_DV
CONV_W = 4
CONV_CH = 2 * GDN_QK + GDN_V
D_IN = 2 * GDN_QK + 2 * GDN_V + 2 * GDN_HEADS + 2 * ML_QK + 3 * ML_V + 2 * ML_HEADS
EPS = 1e-6
FORGET_BIAS = 3.0

kernel_name = 'hymba_gdn_mlstm_stream_step'


def rmsnorm(x, g):
    xf = x.astype(jnp.float32)
    y = xf * lax.rsqrt(jnp.mean(xf * xf, -1, keepdims=True) + EPS) * g.astype(jnp.float32)
    return y.astype(x.dtype)


def head_rms(x):
    return x * lax.rsqrt(jnp.mean(x * x, -1, keepdims=True) + EPS)


def l2norm(x):
    return x * lax.rsqrt(jnp.sum(x * x, -1, keepdims=True) + EPS)


def split_columns(proj):
    sizes = (GDN_QK, GDN_QK, GDN_V, GDN_V, GDN_HEADS, GDN_HEADS,
             ML_QK, ML_QK, ML_V, ML_V, ML_V, ML_HEADS, ML_HEADS)
    idx = np.cumsum(np.array(sizes))[:-1].tolist()
    return jnp.split(proj, idx, axis=-1)


def causal_conv_silu(u, buf, w):
    T = u.shape[1]
    full = jnp.concatenate([buf, u], axis=1)
    y = full[:, 0:T] * w[0]
    for j in range(1, CONV_W):
        y = y + full[:, j:j + T] * w[j]
    return jax.nn.silu(y), full[:, -(CONV_W - 1):]


def to_blocks(a, nb, L):
    B, T, H = a.shape[:3]
    a = a.reshape((B, nb, L, H) + a.shape[3:])
    return jnp.swapaxes(a, 2, 3)


def gated_delta_chunked(q, k, v, g, beta, S0):
    B, T, H, _ = q.shape
    DV = v.shape[-1]
    L = min(CHUNK, T)
    nb = T // L
    q, k, v, g, beta = (to_blocks(a, nb, L) for a in (q, k, v, g, beta))
    G = jnp.cumsum(g, axis=-1)
    strict = jnp.tril(jnp.ones((L, L), bool), -1)
    incl = jnp.tril(jnp.ones((L, L), bool))
    diff = G[..., :, None] - G[..., None, :]
    dec_strict = jnp.exp(jnp.where(strict, diff, -jnp.inf))
    dec_incl = jnp.exp(jnp.where(incl, diff, -jnp.inf))
    A = beta[..., :, None] * jnp.einsum('bnhtd,bnhsd->bnhts', k, k) * dec_strict
    IA = A + jnp.eye(L, dtype=A.dtype)
    eG = jnp.exp(G)
    W = lax.linalg.triangular_solve(IA, (beta * eG)[..., None] * k, left_side=True, lower=True, unit_diagonal=True)
    Ut = lax.linalg.triangular_solve(IA, beta[..., None] * v, left_side=True, lower=True, unit_diagonal=True)
    QK = jnp.einsum('bnhtd,bnhsd->bnhts', q, k) * dec_incl
    Kd = k * jnp.exp(G[..., -1:] - G)[..., None]
    eGL = eG[..., -1]
    xs = tuple(jnp.moveaxis(a, 1, 0) for a in (q, W, Ut, QK, Kd, eG, eGL))

    def step(S, inp):
        qc, Wc, Uc, QKc, Kdc, eGc, eGLc = inp
        U = Uc - jnp.einsum('bhlk,bhkv->bhlv', Wc, S)
        o = eGc[..., None] * jnp.einsum('bhlk,bhkv->bhlv', qc, S) + jnp.einsum('bhts,bhsv->bhtv', QKc, U)
        S = eGLc[..., None, None] * S + jnp.einsum('bhlk,bhlv->bhkv', Kdc, U)
        return S, o

    S, o = lax.scan(step, S0, xs)
    o = jnp.transpose(o, (1, 0, 3, 2, 4)).reshape(B, T, H, DV)
    return o, S


def mlstm_chunked(q, k, v, logi, logf, C0, n0, m0):
    B, T, H, _ = q.shape
    DV = v.shape[-1]
    L = min(CHUNK, T)
    nb = T // L
    q, k, v, logi, logf = (to_blocks(a, nb, L) for a in (q, k, v, logi, logf))
    b = jnp.cumsum(logf, axis=-1)
    incl = jnp.tril(jnp.ones((L, L), bool))
    D = jnp.where(incl, b[..., :, None] - b[..., None, :] + logi[..., None, :], -jnp.inf)
    Dmax = jnp.max(D, axis=-1)
    QK = jnp.einsum('bnhtd,bnhsd->bnhts', q, k)
    to_end = b[..., -1:] - b + logi
    bL = b[..., -1]
    xs = tuple(jnp.moveaxis(a, 1, 0) for a in (q, k, v, b, D, Dmax, QK, to_end, bL))

    def step(carry, inp):
        C, n, m = carry
        qc, kc, vc, bc, Dc, Dmc, QKc, endc, bLc = inp
        inter = bc + m[..., None]
        mt = jnp.maximum(inter, Dmc)
        P = jnp.exp(Dc - mt[..., None]) * QKc
        wq = jnp.exp(inter - mt)
        num = wq[..., None] * jnp.einsum('bhlk,bhkv->bhlv', qc, C) + jnp.einsum('bhts,bhsv->bhtv', P, vc)
        den = wq * jnp.einsum('bhlk,bhk->bhl', qc, n) + jnp.sum(P, axis=-1)
        h = num / jnp.maximum(jnp.abs(den), jnp.exp(-mt))[..., None]
        m_new = mt[..., -1]
        wC = jnp.exp(bLc + m - m_new)
        ws = jnp.exp(endc - m_new[..., None])
        C = wC[..., None, None] * C + jnp.einsum('bhl,bhlk,bhlv->bhkv', ws, kc, vc)
        n = wC[..., None] * n + jnp.einsum('bhl,bhlk->bhk', ws, kc)
        return (C, n, m_new), h

    (C, n, m), h = lax.scan(step, (C0, n0, m0), xs)
    h = jnp.transpose(h, (1, 0, 3, 2, 4)).reshape(B, T, H, DV)
    return h, C, n, m


def mixer_layer(x, state, norm_g, w_in, conv_w, a_log, dt_bias, gdn_norm_w, b_i, b_f, ml_norm_w, w_out):
    f32 = jnp.float32
    conv_buf, S0, C0, n0, m0 = state
    Bsz, T, _ = x.shape
    h = rmsnorm(x, norm_g)
    proj = jnp.einsum('btd,de->bte', h, w_in).astype(f32)
    gq, gk, gv, gz, gb, ga, mq, mk, mv, mo, mz, mi, mf = split_columns(proj)
    qkv, conv_new = causal_conv_silu(jnp.concatenate([gq, gk, gv], -1), conv_buf.astype(f32), conv_w.astype(f32))
    q, k, v = jnp.split(qkv, [GDN_QK, 2 * GDN_QK], axis=-1)
    q = l2norm(q.reshape(Bsz, T, GDN_HEADS, GDN_DK)) * (GDN_DK ** -0.5)
    k = l2norm(k.reshape(Bsz, T, GDN_HEADS, GDN_DK))
    v = v.reshape(Bsz, T, GDN_HEADS, GDN_DV)
    beta = jax.nn.sigmoid(gb)
    g = -jnp.exp(a_log.astype(f32)) * jax.nn.softplus(ga + dt_bias.astype(f32))
    o, S_new = gated_delta_chunked(q, k, v, g, beta, S0.astype(f32))
    gdn_out = (head_rms(o) * gdn_norm_w.astype(f32) * jax.nn.silu(gz.reshape(Bsz, T, GDN_HEADS, GDN_DV))).reshape(Bsz, T, GDN_V)
    mq = mq.reshape(Bsz, T, ML_HEADS, ML_DQK)
    mk = mk.reshape(Bsz, T, ML_HEADS, ML_DQK) * (ML_DQK ** -0.5)
    mv = mv.reshape(Bsz, T, ML_HEADS, ML_DV)
    logi = mi + b_i.astype(f32)
    logf = jax.nn.log_sigmoid(mf + b_f.astype(f32))
    hc, C_new, n_new, m_new = mlstm_chunked(mq, mk, mv, logi, logf, C0.astype(f32), n0.astype(f32), m0.astype(f32))
    ml_out = (head_rms(hc) * ml_norm_w.astype(f32)).reshape(Bsz, T, ML_V) * jax.nn.sigmoid(mo) * jax.nn.silu(mz)
    mixed = jnp.concatenate([gdn_out, ml_out], axis=-1).astype(x.dtype)
    x = x + jnp.einsum('bte,ed->btd', mixed, w_out)
    return x, (conv_new, S_new, C_new, n_new, m_new)


def setup_inputs(seed: int = 0) -> dict:
    key = jax.random.key(seed)
    ks = jax.random.split(key, 18)
    f32 = jnp.float32
    nrm = lambda kk, s: jax.random.normal(kk, s, f32)
    dt = jnp.exp(jax.random.uniform(ks[11], (DEPTH, GDN_HEADS), f32, math.log(1e-3), math.log(1e-1)))
    return {
        'x_prompt': nrm(ks[0], (BATCH, SEQ, D_MODEL)),
        'x_sample': nrm(ks[1], (DEC_BATCH, DEC_SEQ, D_MODEL)),
        'cache_gdn_conv': nrm(ks[2], (DEPTH, DEC_BATCH, CONV_W - 1, CONV_CH)),
        'state_gdn': 0.05 * nrm(ks[3], (DEPTH, DEC_BATCH, GDN_HEADS, GDN_DK, GDN_DV)),
        'state_mlstm_c': 0.05 * nrm(ks[4], (DEPTH, DEC_BATCH, ML_HEADS, ML_DQK, ML_DV)),
        'state_mlstm_n': 0.1 * nrm(ks[5], (DEPTH, DEC_BATCH, ML_HEADS, ML_DQK)),
        'state_mlstm_m': nrm(ks[6], (DEPTH, DEC_BATCH, ML_HEADS)),
        'norm_g': 1.0 + 0.02 * nrm(ks[7], (DEPTH, D_MODEL)),
        'w_in': nrm(ks[8], (DEPTH, D_MODEL, D_IN)) * (D_MODEL ** -0.5),
        'conv_w': nrm(ks[9], (DEPTH, CONV_W, CONV_CH)) * (CONV_W ** -0.5),
        'gdn_a_log': jnp.log(jax.random.uniform(ks[10], (DEPTH, GDN_HEADS), f32, 1.0, 16.0)),
        'gdn_dt_bias': dt + jnp.log(-jnp.expm1(-dt)),
        'gdn_norm_w': 1.0 + 0.02 * nrm(ks[12], (DEPTH, GDN_DV)),
        'mlstm_b_i': 0.01 * nrm(ks[13], (DEPTH, ML_HEADS)),
        'mlstm_b_f': FORGET_BIAS + 0.1 * nrm(ks[14], (DEPTH, ML_HEADS)),
        'mlstm_norm_w': 1.0 + 0.02 * nrm(ks[15], (DEPTH, ML_DV)),
        'w_out': nrm(ks[16], (DEPTH, D_MIX, D_MODEL)) * (D_MIX ** -0.5),
        'final_g': 1.0 + 0.02 * nrm(ks[17], (D_MODEL,)),
    }


def reference(x_prompt, x_sample, cache_gdn_conv, state_gdn, state_mlstm_c, state_mlstm_n, state_mlstm_m,
              norm_g, w_in, conv_w, gdn_a_log, gdn_dt_bias, gdn_norm_w, mlstm_b_i, mlstm_b_f, mlstm_norm_w,
              w_out, final_g):
    f32 = jnp.float32
    B = x_prompt.shape[0]
    xp, xs = x_prompt, x_sample
    p_st, s_st = [], []
    for l in range(DEPTH):
        lw = (norm_g[l], w_in[l], conv_w[l], gdn_a_log[l], gdn_dt_bias[l], gdn_norm_w[l],
              mlstm_b_i[l], mlstm_b_f[l], mlstm_norm_w[l], w_out[l])
        fresh = (jnp.zeros((B, CONV_W - 1, CONV_CH), f32),
                 jnp.zeros((B, GDN_HEADS, GDN_DK, GDN_DV), f32),
                 jnp.zeros((B, ML_HEADS, ML_DQK, ML_DV), f32),
                 jnp.zeros((B, ML_HEADS, ML_DQK), f32),
                 jnp.zeros((B, ML_HEADS), f32))
        carried = (cache_gdn_conv[l], state_gdn[l], state_mlstm_c[l], state_mlstm_n[l], state_mlstm_m[l])
        xp, sp = mixer_layer(xp, fresh, *lw)
        xs, ss = mixer_layer(xs, carried, *lw)
        p_st.append(sp)
        s_st.append(ss)
    y_prompt = rmsnorm(xp, final_g)
    y_sample = rmsnorm(xs, final_g)
    stk = lambda sts, i: jnp.stack([s[i] for s in sts])
    return (y_prompt, y_sample,
            stk(p_st, 0), stk(p_st, 1), stk(p_st, 2), stk(p_st, 3), stk(p_st, 4),
            stk(s_st, 0), stk(s_st, 1), stk(s_st, 2), stk(s_st, 3), stk(s_st, 4))
```

```python
import functools

import jax
import jax.numpy as jnp
from jax import lax
from jax.experimental import pallas as pl
from jax.experimental.pallas import tpu as pltpu

D_MODEL = 4096
CHUNK = 64
GDN_DK = 128
GDN_DV = 128
GDN_HEADS = 16
GDN_QK = GDN_HEADS * GDN_DK
GDN_V = GDN_HEADS * GDN_DV
ML_HEADS = 8
ML_DV = 256
ML_DQK = 128
ML_QK = ML_HEADS * ML_DQK
ML_V = ML_HEADS * ML_DV
CONV_W = 4
CONV_CH = 2 * GDN_QK + GDN_V
EPS = 1e-6

GDN_WIDE = 2 * GDN_QK + 2 * GDN_V
ML_WIDE = 2 * ML_QK + 3 * ML_V
WIDE = GDN_WIDE + ML_WIDE
LANES = 128
GB_LANE = 0
GA_LANE = GDN_HEADS
MI_LANE = 2 * GDN_HEADS
MF_LANE = 2 * GDN_HEADS + ML_HEADS

VMEM_LIMIT = 48 * 1024 * 1024
HI = lax.Precision.HIGHEST
BF = jnp.bfloat16
F32 = jnp.float32


def _dot(a, b):
    return jnp.dot(a.astype(BF), b.astype(BF), preferred_element_type=F32)


def _dot_nt(a, b):
    return lax.dot_general(a.astype(BF), b.astype(BF), (((1,), (1,)), ((), ())), preferred_element_type=F32)


def _dot_tn(a, b):
    return lax.dot_general(a.astype(BF), b.astype(BF), (((0,), (0,)), ((), ())), preferred_element_type=F32)


def _dot_hi(a, b):
    return jnp.dot(a, b, precision=HI, preferred_element_type=F32)


def _sigmoid(x):
    return 1.0 / (1.0 + jnp.exp(-x))


def _silu(x):
    return x * _sigmoid(x)


def _softplus(x):
    return jnp.maximum(x, 0.0) + jnp.log(1.0 + jnp.exp(-jnp.abs(x)))


def _log_sigmoid(x):
    return -_softplus(-x)


def _prep_kernel(x_ref, g_ref, wg_ref, h_ref, gate_ref):
    x = x_ref[...]
    h = x * lax.rsqrt(jnp.mean(x * x, axis=-1, keepdims=True) + EPS) * g_ref[...]
    hb = h.astype(BF)
    h_ref[...] = hb
    gate_ref[...] = jnp.dot(hb, wg_ref[...], preferred_element_type=F32)


def _prep(x2d, norm_g, w_gate, tm=512):
    m = x2d.shape[0]
    return pl.pallas_call(
        _prep_kernel,
        grid=(m // tm,),
        in_specs=[pl.BlockSpec((tm, D_MODEL), lambda i: (i, 0)),
                  pl.BlockSpec((1, D_MODEL), lambda i: (0, 0)),
                  pl.BlockSpec((D_MODEL, LANES), lambda i: (0, 0))],
        out_specs=[pl.BlockSpec((tm, D_MODEL), lambda i: (i, 0)),
                   pl.BlockSpec((tm, LANES), lambda i: (i, 0))],
        out_shape=[jax.ShapeDtypeStruct((m, D_MODEL), BF),
                   jax.ShapeDtypeStruct((m, LANES), F32)],
        compiler_params=pltpu.CompilerParams(dimension_semantics=("parallel",), vmem_limit_bytes=VMEM_LIMIT),
        name="prep",
    )(x2d, norm_g, w_gate)


def _inproj_kernel(h_ref, w_ref, o_ref):
    o_ref[...] = jnp.dot(h_ref[...], w_ref[...], preferred_element_type=F32)


def _inproj(h, w_wide, tm=1024, tn=512):
    m = h.shape[0]
    n = w_wide.shape[1]
    return pl.pallas_call(
        _inproj_kernel,
        grid=(m // tm, n // tn),
        in_specs=[pl.BlockSpec((tm, D_MODEL), lambda i, j: (i, 0)),
                  pl.BlockSpec((D_MODEL, tn), lambda i, j: (0, j))],
        out_specs=pl.BlockSpec((tm, tn), lambda i, j: (i, j)),
        out_shape=jax.ShapeDtypeStruct((m, n), F32),
        compiler_params=pltpu.CompilerParams(dimension_semantics=("parallel", "arbitrary"),
                                             vmem_limit_bytes=VMEM_LIMIT),
        name="in_proj",
    )(h, w_wide)


def _tri_masks():
    r = lax.broadcasted_iota(jnp.int32, (CHUNK, CHUNK), 0)
    c = lax.broadcasted_iota(jnp.int32, (CHUNK, CHUNK), 1)
    return r, c


def _unit_lower_inverse(a, r, c):
    eye = (r == c).astype(F32)
    d = eye - jnp.where((r >> 1) == (c >> 1), a, 0.0)
    lvl = 1
    while (1 << lvl) < CHUNK:
        off = jnp.where(((r >> (lvl + 1)) == (c >> (lvl + 1))) & ((r >> lvl) != (c >> lvl)), a, 0.0)
        d = d - _dot_hi(_dot_hi(d, off), d)
        lvl += 1
    return d


def _gdn_kernel(qkv_ref, z_ref, gate_ref, conv0_ref, s0_ref, convw_ref, alog_ref, dtb_ref, normw_ref,
                out_ref, sout_ref, convout_ref, xbuf):
    ci = pl.program_id(1)
    tail = CONV_W - 1

    @pl.when(ci == 0)
    def _():
        xbuf[8 - tail:8, :] = conv0_ref[0]
        sout_ref[...] = s0_ref[...]

    xbuf[8:8 + CHUNK, :] = qkv_ref[...]
    convout_ref[0] = qkv_ref[CHUNK - tail:CHUNK, :]

    gate = gate_ref[...]
    beta_all = _sigmoid(gate)
    g_all = -jnp.exp(alog_ref[...]) * _softplus(gate + dtb_ref[...])
    r, c = _tri_masks()
    tril = (r >= c).astype(F32)
    gcum = _dot_hi(tril, g_all)
    gcum_t = gcum.T
    incl = r >= c
    strict = r > c

    def conv_silu(col):
        y = xbuf[5:5 + CHUNK, col:col + LANES] * convw_ref[0:1, col:col + LANES]
        for j in range(1, CONV_W):
            y = y + xbuf[5 + j:5 + j + CHUNK, col:col + LANES] * convw_ref[j:j + 1, col:col + LANES]
        return _silu(y)

    for h in range(GDN_HEADS):
        q = conv_silu(h * GDN_DK)
        k = conv_silu(GDN_QK + h * GDN_DK)
        v = conv_silu(2 * GDN_QK + h * GDN_DV)
        q = q * lax.rsqrt(jnp.sum(q * q, axis=-1, keepdims=True) + EPS) * (GDN_DK ** -0.5)
        k = k * lax.rsqrt(jnp.sum(k * k, axis=-1, keepdims=True) + EPS)
        gc = gcum[:, GA_LANE + h:GA_LANE + h + 1]
        gr = gcum_t[GA_LANE + h:GA_LANE + h + 1, :]
        bc = beta_all[:, GB_LANE + h:GB_LANE + h + 1]
        dec_incl = jnp.exp(jnp.where(incl, gc - gr, -jnp.inf))
        dec_strict = jnp.where(strict, dec_incl, 0.0)
        a = bc * _dot_nt(k, k) * dec_strict
        tinv = _unit_lower_inverse(a, r, c)
        eg = jnp.exp(gc)
        rhs = jnp.concatenate([(bc * eg) * k, bc * v], axis=1)
        sol = _dot_hi(tinv, rhs)
        w = sol[:, :GDN_DK]
        ut = sol[:, GDN_DK:]
        qk = _dot_nt(q, k) * dec_incl
        gl = gc[CHUNK - 1:CHUNK, :]
        kd = k * jnp.exp(gl - gc)
        s = sout_ref[0, h]
        u = ut - _dot(w, s)
        o = eg * _dot(q, s) + _dot(qk, u)
        sout_ref[0, h] = jnp.exp(gl) * s + _dot_tn(kd, u)
        on = o * lax.rsqrt(jnp.mean(o * o, axis=-1, keepdims=True) + EPS) * normw_ref[...]
        zh = z_ref[:, h * GDN_DV:(h + 1) * GDN_DV]
        out_ref[:, h * GDN_DV:(h + 1) * GDN_DV] = (on * _silu(zh)).astype(out_ref.dtype)

    xbuf[8 - tail:8, :] = xbuf[8 + CHUNK - tail:8 + CHUNK, :]


def _gdn(proj, gates, conv0, s0, conv_w, alog_row, dtb_row, normw_row, nseq, nchunk):
    m = proj.shape[0]
    row = lambda b, ci: (b * nchunk + ci, 0)
    return pl.pallas_call(
        _gdn_kernel,
        grid=(nseq, nchunk),
        in_specs=[pl.BlockSpec((CHUNK, CONV_CH), row),
                  pl.BlockSpec((CHUNK, GDN_V), lambda b, ci: (b * nchunk + ci, CONV_CH // GDN_V)),
                  pl.BlockSpec((CHUNK, LANES), row),
                  pl.BlockSpec((1, CONV_W - 1, CONV_CH), lambda b, ci: (b, 0, 0)),
                  pl.BlockSpec((1, GDN_HEADS, GDN_DK, GDN_DV), lambda b, ci: (b, 0, 0, 0)),
                  pl.BlockSpec((CONV_W, CONV_CH), lambda b, ci: (0, 0)),
                  pl.BlockSpec((1, LANES), lambda b, ci: (0, 0)),
                  pl.BlockSpec((1, LANES), lambda b, ci: (0, 0)),
                  pl.BlockSpec((1, GDN_DV), lambda b, ci: (0, 0))],
        out_specs=[pl.BlockSpec((CHUNK, GDN_V), row),
                   pl.BlockSpec((1, GDN_HEADS, GDN_DK, GDN_DV), lambda b, ci: (b, 0, 0, 0)),
                   pl.BlockSpec((1, CONV_W - 1, CONV_CH), lambda b, ci: (b, 0, 0))],
        out_shape=[jax.ShapeDtypeStruct((m, GDN_V), BF),
                   jax.ShapeDtypeStruct((nseq, GDN_HEADS, GDN_DK, GDN_DV), F32),
                   jax.ShapeDtypeStruct((nseq, CONV_W - 1, CONV_CH), F32)],
        scratch_shapes=[pltpu.VMEM((8 + CHUNK, CONV_CH), F32)],
        compiler_params=pltpu.CompilerParams(dimension_semantics=("parallel", "arbitrary"),
                                             vmem_limit_bytes=VMEM_LIMIT),
        name="gdn",
    )(proj, proj, gates, conv0, s0, conv_w, alog_row, dtb_row, normw_row)


def _mlstm_kernel(qk_ref, v_ref, o_ref, z_ref, gate_ref, c0_ref, n0_ref, m0_ref, bi_ref, bf_ref, normw_ref,
                  out_ref, cout_ref, nout_ref, mout_ref):
    ci = pl.program_id(1)

    @pl.when(ci == 0)
    def _():
        cout_ref[...] = c0_ref[...]
        nout_ref[...] = n0_ref[...]
        mout_ref[...] = m0_ref[...]

    gate = gate_ref[...]
    logi_all = gate + bi_ref[...]
    logf_all = _log_sigmoid(gate + bf_ref[...])
    r, c = _tri_masks()
    incl = r >= c
    bcum = _dot_hi(incl.astype(F32), logf_all)
    bcum_t = bcum.T
    logi_t = logi_all.T
    m_row = mout_ref[0]
    lane = lax.broadcasted_iota(jnp.int32, (1, LANES), 1)
    m_new_row = m_row

    for h in range(ML_HEADS):
        q = qk_ref[:, h * ML_DQK:(h + 1) * ML_DQK]
        k = qk_ref[:, ML_QK + h * ML_DQK:ML_QK + (h + 1) * ML_DQK] * (ML_DQK ** -0.5)
        v = v_ref[:, h * ML_DV:(h + 1) * ML_DV]
        b_c = bcum[:, MF_LANE + h:MF_LANE + h + 1]
        b_r = bcum_t[MF_LANE + h:MF_LANE + h + 1, :]
        li_c = logi_all[:, MI_LANE + h:MI_LANE + h + 1]
        li_r = logi_t[MI_LANE + h:MI_LANE + h + 1, :]
        m_prev = m_row[:, MF_LANE + h:MF_LANE + h + 1]
        dmat = jnp.where(incl, b_c - b_r + li_r, -jnp.inf)
        dmax = jnp.max(dmat, axis=-1, keepdims=True)
        qk = _dot_nt(q, k)
        inter = b_c + m_prev
        mt = jnp.maximum(inter, dmax)
        p = jnp.exp(dmat - mt) * qk
        wq = jnp.exp(inter - mt)
        cst = cout_ref[0, h]
        nst = nout_ref[0, h:h + 1, :]
        num = wq * _dot(q, cst) + _dot(p, v)
        qn = jnp.sum(q * nst, axis=-1, keepdims=True)
        den = wq * qn + jnp.sum(p, axis=-1, keepdims=True)
        hh = num / jnp.maximum(jnp.abs(den), jnp.exp(-mt))
        m_new = mt[CHUNK - 1:CHUNK, :]
        b_l = b_c[CHUNK - 1:CHUNK, :]
        wc = jnp.exp(b_l + m_prev - m_new)
        ws = jnp.exp(b_l - b_c + li_c - m_new)
        cout_ref[0, h] = wc * cst + _dot_tn(k, ws * v)
        ksum = jnp.sum(ws * k, axis=0, keepdims=True)
        nout_ref[0, h:h + 1, :] = wc * nst + ksum
        m_new_row = jnp.where(lane == MF_LANE + h, m_new, m_new_row)
        hn = hh * lax.rsqrt(jnp.mean(hh * hh, axis=-1, keepdims=True) + EPS) * normw_ref[...]
        og = o_ref[:, h * ML_DV:(h + 1) * ML_DV]
        zg = z_ref[:, h * ML_DV:(h + 1) * ML_DV]
        out_ref[:, h * ML_DV:(h + 1) * ML_DV] = (hn * _sigmoid(og) * _silu(zg)).astype(out_ref.dtype)

    mout_ref[0] = m_new_row


def _mlstm(proj, gates, c0, n0, m0, bi_row, bf_row, normw_row, nseq, nchunk):
    m = proj.shape[0]
    base = GDN_WIDE // ML_V
    blk = lambda j: (lambda b, ci: (b * nchunk + ci, base + j))
    return pl.pallas_call(
        _mlstm_kernel,
        grid=(nseq, nchunk),
        in_specs=[pl.BlockSpec((CHUNK, 2 * ML_QK), blk(0)),
                  pl.BlockSpec((CHUNK, ML_V), blk(1)),
                  pl.BlockSpec((CHUNK, ML_V), blk(2)),
                  pl.BlockSpec((CHUNK, ML_V), blk(3)),
                  pl.BlockSpec((CHUNK, LANES), lambda b, ci: (b * nchunk + ci, 0)),
                  pl.BlockSpec((1, ML_HEADS, ML_DQK, ML_DV), lambda b, ci: (b, 0, 0, 0)),
                  pl.BlockSpec((1, ML_HEADS, ML_DQK), lambda b, ci: (b, 0, 0)),
                  pl.BlockSpec((1, 1, LANES), lambda b, ci: (b, 0, 0)),
                  pl.BlockSpec((1, LANES), lambda b, ci: (0, 0)),
                  pl.BlockSpec((1, LANES), lambda b, ci: (0, 0)),
                  pl.BlockSpec((1, ML_DV), lambda b, ci: (0, 0))],
        out_specs=[pl.BlockSpec((CHUNK, ML_V), lambda b, ci: (b * nchunk + ci, 0)),
                   pl.BlockSpec((1, ML_HEADS, ML_DQK, ML_DV), lambda b, ci: (b, 0, 0, 0)),
                   pl.BlockSpec((1, ML_HEADS, ML_DQK), lambda b, ci: (b, 0, 0)),
                   pl.BlockSpec((1, 1, LANES), lambda b, ci: (b, 0, 0))],
        out_shape=[jax.ShapeDtypeStruct((m, ML_V), BF),
                   jax.ShapeDtypeStruct((nseq, ML_HEADS, ML_DQK, ML_DV), F32),
                   jax.ShapeDtypeStruct((nseq, ML_HEADS, ML_DQK), F32),
                   jax.ShapeDtypeStruct((nseq, 1, LANES), F32)],
        compiler_params=pltpu.CompilerParams(dimension_semantics=("parallel", "arbitrary"),
                                             vmem_limit_bytes=VMEM_LIMIT),
        name="mlstm",
    )(proj, proj, proj, proj, gates, c0, n0, m0, bi_row, bf_row, normw_row)


def _out_kernel(a_ref, b_ref, wa_ref, wb_ref, x_ref, g_ref, y_ref):
    ki = pl.program_id(1)
    part = (jnp.dot(a_ref[...], wa_ref[...], preferred_element_type=F32)
            + jnp.dot(b_ref[...], wb_ref[...], preferred_element_type=F32))

    @pl.when(ki == 0)
    def _():
        y_ref[...] = x_ref[...] + part

    @pl.when(ki != 0)
    def _():
        y_ref[...] += part

    @pl.when(ki == pl.num_programs(1) - 1)
    def _():
        y = y_ref[...]
        y_ref[...] = y * lax.rsqrt(jnp.mean(y * y, axis=-1, keepdims=True) + EPS) * g_ref[...]


def _outproj(gdn_o, ml_o, w_a, w_b, x2d, final_g, tm=256, tk=512):
    m = x2d.shape[0]
    return pl.pallas_call(
        _out_kernel,
        grid=(m // tm, GDN_V // tk),
        in_specs=[pl.BlockSpec((tm, tk), lambda i, k: (i, k)),
                  pl.BlockSpec((tm, tk), lambda i, k: (i, k)),
                  pl.BlockSpec((tk, D_MODEL), lambda i, k: (k, 0)),
                  pl.BlockSpec((tk, D_MODEL), lambda i, k: (k, 0)),
                  pl.BlockSpec((tm, D_MODEL), lambda i, k: (i, 0)),
                  pl.BlockSpec((1, D_MODEL), lambda i, k: (0, 0))],
        out_specs=pl.BlockSpec((tm, D_MODEL), lambda i, k: (i, 0)),
        out_shape=jax.ShapeDtypeStruct((m, D_MODEL), F32),
        compiler_params=pltpu.CompilerParams(dimension_semantics=("parallel", "arbitrary"),
                                             vmem_limit_bytes=VMEM_LIMIT),
        name="out_proj",
    )(gdn_o, ml_o, w_a, w_b, x2d, final_g)


def _lane_row(vec, lane0):
    return jnp.zeros((1, LANES), F32).at[0, lane0:lane0 + vec.shape[0]].set(vec.astype(F32))


def _layer(x, state, wts):
    (w_wide, w_gate, norm_g, conv_w, alog_row, dtb_row, gdn_normw, bi_row, bf_row, ml_normw,
     w_out_a, w_out_b, final_g) = wts
    conv0, s0, c0, n0, m0 = state
    nseq, t, _ = x.shape
    nchunk = t // CHUNK
    x2d = x.reshape(nseq * t, D_MODEL)
    h, gates = _prep(x2d, norm_g, w_gate)
    proj = _inproj(h, w_wide)
    gdn_o, s_new, conv_new = _gdn(proj, gates, conv0, s0, conv_w, alog_row, dtb_row, gdn_normw, nseq, nchunk)
    m0_row = jnp.zeros((nseq, 1, LANES), F32).at[:, 0, MF_LANE:MF_LANE + ML_HEADS].set(m0)
    ml_o, c_new, n_new, m_new_row = _mlstm(proj, gates, c0, n0, m0_row, bi_row, bf_row, ml_normw, nseq, nchunk)
    y = _outproj(gdn_o, ml_o, w_out_a, w_out_b, x2d, final_g)
    m_new = m_new_row[:, 0, MF_LANE:MF_LANE + ML_HEADS]
    return y.reshape(nseq, t, D_MODEL), (conv_new, s_new, c_new, n_new, m_new)


def kernel(x_prompt, x_sample, cache_gdn_conv, state_gdn, state_mlstm_c, state_mlstm_n, state_mlstm_m,
           norm_g, w_in, conv_w, gdn_a_log, gdn_dt_bias, gdn_norm_w, mlstm_b_i, mlstm_b_f, mlstm_norm_w,
           w_out, final_g):
    assert w_in.shape[0] == 1, "single-layer model"
    w = w_in[0]
    g_end = GDN_WIDE + 2 * GDN_HEADS
    m_end = g_end + ML_WIDE
    w_wide = jnp.concatenate([w[:, :GDN_WIDE], w[:, g_end:m_end]], axis=1).astype(BF)
    w_gate = jnp.concatenate(
        [w[:, GDN_WIDE:g_end], w[:, m_end:], jnp.zeros((D_MODEL, LANES - 2 * GDN_HEADS - 2 * ML_HEADS), F32)],
        axis=1).astype(BF)
    wts = (w_wide, w_gate, norm_g[0][None, :], conv_w[0],
           _lane_row(gdn_a_log[0], GA_LANE), _lane_row(gdn_dt_bias[0], GA_LANE), gdn_norm_w[0][None, :],
           _lane_row(mlstm_b_i[0], MI_LANE), _lane_row(mlstm_b_f[0], MF_LANE), mlstm_norm_w[0][None, :],
           w_out[0, :GDN_V].astype(BF), w_out[0, GDN_V:].astype(BF), final_g[None, :])
    nb = x_prompt.shape[0]
    fresh = (jnp.zeros((nb, CONV_W - 1, CONV_CH), F32),
             jnp.zeros((nb, GDN_HEADS, GDN_DK, GDN_DV), F32),
             jnp.zeros((nb, ML_HEADS, ML_DQK, ML_DV), F32),
             jnp.zeros((nb, ML_HEADS, ML_DQK), F32),
             jnp.zeros((nb, ML_HEADS), F32))
    carried = (cache_gdn_conv[0], state_gdn[0], state_mlstm_c[0], state_mlstm_n[0], state_mlstm_m[0])
    yp, sp = _layer(x_prompt, fresh, wts)
    ys, ss = _layer(x_sample, carried, wts)
    return (yp, ys) + tuple(a[None] for a in sp) + tuple(a[None] for a in ss)
```

```python
import jax
import jax.numpy as jnp
import numpy as np
from jax import lax
from jax.experimental import pallas as pl
from jax.experimental.pallas import tpu as pltpu

D_MODEL = 4096
CHUNK = 64
GDN_DK = 128
GDN_DV = 128
GDN_HEADS = 16
GDN_PAIRS = GDN_HEADS // 2
GDN_QK = GDN_HEADS * GDN_DK
GDN_V = GDN_HEADS * GDN_DV
ML_HEADS = 8
ML_DV = 256
ML_DQK = 128
ML_QK = ML_HEADS * ML_DQK
ML_V = ML_HEADS * ML_DV
CONV_W = 4
CONV_CH = 2 * GDN_QK + GDN_V
EPS = 1e-6

GDN_WIDE = 2 * GDN_QK + 2 * GDN_V
ML_WIDE = 2 * ML_QK + 3 * ML_V
WIDE = GDN_WIDE + ML_WIDE
LANES = 128
GB_LANE = 0
GA_LANE = GDN_HEADS
MI_LANE = 2 * GDN_HEADS
MF_LANE = 2 * GDN_HEADS + ML_HEADS
GDN_HEAD_ORDER = np.concatenate([np.arange(0, GDN_HEADS, 2), np.arange(1, GDN_HEADS, 2)])

VMEM_LIMIT = 48 * 1024 * 1024
OUT_VMEM_LIMIT = 60 * 1024 * 1024
HI = lax.Precision.HIGHEST
BF = jnp.bfloat16
F32 = jnp.float32


def _gdn_lane(h):
    return (h % 2) * GDN_PAIRS + h // 2


def _dot(a, b):
    return jnp.dot(a.astype(BF), b.astype(BF), preferred_element_type=F32)


def _dot_nt(a, b):
    return lax.dot_general(a.astype(BF), b.astype(BF), (((1,), (1,)), ((), ())), preferred_element_type=F32)


def _dot_tn(a, b):
    return lax.dot_general(a.astype(BF), b.astype(BF), (((0,), (0,)), ((), ())), preferred_element_type=F32)


def _dot_hi(a, b):
    return jnp.dot(a, b, precision=HI, preferred_element_type=F32)


def _split(x):
    hi = x.astype(BF)
    lo = (x - hi.astype(F32)).astype(BF)
    return hi, lo


def _dot_3pass(a, b):
    a_hi, a_lo = _split(a)
    b_hi, b_lo = _split(b)
    return jnp.dot(jnp.concatenate([a_hi, a_lo, a_hi], axis=1), jnp.concatenate([b_hi, b_hi, b_lo], axis=0),
                   preferred_element_type=F32)


def _sigmoid(x):
    return 1.0 / (1.0 + jnp.exp(-x))


def _silu(x):
    return x * _sigmoid(x)


def _softplus(x):
    return jnp.maximum(x, 0.0) + jnp.log(1.0 + jnp.exp(-jnp.abs(x)))


def _log_sigmoid(x):
    return -_softplus(-x)


def _prep_kernel(x_ref, g_ref, wg_ref, h_ref, gate_ref):
    x = x_ref[...]
    h = x * lax.rsqrt(jnp.mean(x * x, axis=-1, keepdims=True) + EPS) * g_ref[...]
    hb = h.astype(BF)
    h_ref[...] = hb
    gate_ref[...] = jnp.dot(hb, wg_ref[...], preferred_element_type=F32)


def _prep(x2d, norm_g, w_gate, tm=512):
    m = x2d.shape[0]
    return pl.pallas_call(
        _prep_kernel,
        grid=(m // tm,),
        in_specs=[pl.BlockSpec((tm, D_MODEL), lambda i: (i, 0)),
                  pl.BlockSpec((1, D_MODEL), lambda i: (0, 0)),
                  pl.BlockSpec((D_MODEL, LANES), lambda i: (0, 0))],
        out_specs=[pl.BlockSpec((tm, D_MODEL), lambda i: (i, 0)),
                   pl.BlockSpec((tm, LANES), lambda i: (i, 0))],
        out_shape=[jax.ShapeDtypeStruct((m, D_MODEL), BF),
                   jax.ShapeDtypeStruct((m, LANES), F32)],
        compiler_params=pltpu.CompilerParams(dimension_semantics=("parallel",), vmem_limit_bytes=VMEM_LIMIT),
        name="prep",
    )(x2d, norm_g, w_gate)


def _inproj_kernel(h_ref, w_ref, o_ref):
    o_ref[...] = jnp.dot(h_ref[...], w_ref[...], preferred_element_type=F32)


def _inproj(h, w_wide, tm=1024, tn=512):
    m = h.shape[0]
    n = w_wide.shape[1]
    return pl.pallas_call(
        _inproj_kernel,
        grid=(m // tm, n // tn),
        in_specs=[pl.BlockSpec((tm, D_MODEL), lambda i, j: (i, 0)),
                  pl.BlockSpec((D_MODEL, tn), lambda i, j: (0, j))],
        out_specs=pl.BlockSpec((tm, tn), lambda i, j: (i, j)),
        out_shape=jax.ShapeDtypeStruct((m, n), F32),
        compiler_params=pltpu.CompilerParams(dimension_semantics=("parallel", "arbitrary"),
                                             vmem_limit_bytes=VMEM_LIMIT),
        name="in_proj",
    )(h, w_wide)


def _block_diag(x, left):
    return jnp.concatenate([jnp.where(left, x, 0.0), jnp.where(left, 0.0, x)], axis=0)


def _block_diag_wide(x0, x1):
    z = jnp.zeros_like(x0)
    return jnp.concatenate([jnp.concatenate([x0, z], axis=1), jnp.concatenate([z, x1], axis=1)], axis=0)


def _pair_diag_blocks(m, left):
    return jnp.where(left, m[:CHUNK], m[CHUNK:])


def _gdn_kernel(qkv_ref, z_ref, gate_ref, conv0_ref, s0_ref, convw_ref, alog_ref, dtb_ref, normw_ref,
                out_ref, sout_ref, convout_ref, xbuf):
    ci = pl.program_id(1)
    tail = CONV_W - 1

    @pl.when(ci == 0)
    def _():
        xbuf[8 - tail:8, :] = conv0_ref[0]
        sout_ref[...] = s0_ref[...]

    xbuf[8:8 + CHUNK, :] = qkv_ref[...]
    convout_ref[0] = qkv_ref[CHUNK - tail:CHUNK, :]

    r = lax.broadcasted_iota(jnp.int32, (CHUNK, LANES), 0)
    c = lax.broadcasted_iota(jnp.int32, (CHUNK, LANES), 1)
    left = c < CHUNK
    cc = c & (CHUNK - 1)
    incl = r >= cc
    strict = r > cc

    gate = gate_ref[...]
    beta_all = _sigmoid(gate)
    g_all = -jnp.exp(alog_ref[...]) * _softplus(gate + dtb_ref[...])
    rr = lax.broadcasted_iota(jnp.int32, (CHUNK, CHUNK), 0)
    rc = lax.broadcasted_iota(jnp.int32, (CHUNK, CHUNK), 1)
    gcum = _dot_hi((rr >= rc).astype(F32), g_all)
    comb_t = jnp.where(c < GA_LANE, beta_all, gcum).T
    row_b = jnp.concatenate([comb_t[GB_LANE:GB_LANE + GDN_PAIRS], comb_t[GB_LANE + GDN_PAIRS:GA_LANE]], axis=1)
    row_g = jnp.concatenate([comb_t[GA_LANE:GA_LANE + GDN_PAIRS], comb_t[GA_LANE + GDN_PAIRS:MI_LANE]], axis=1)
    row_beg = row_b * jnp.exp(row_g)

    def conv_silu(col):
        y = xbuf[5:5 + CHUNK, col:col + LANES] * convw_ref[0:1, col:col + LANES]
        for j in range(1, CONV_W):
            y = y + xbuf[5 + j:5 + j + CHUNK, col:col + LANES] * convw_ref[j:j + 1, col:col + LANES]
        return _silu(y)

    def bcast(x, lane):
        return jnp.broadcast_to(x[:, lane:lane + 1], (CHUNK, LANES))

    qs, ks, vs, gcs, a_prs, qk_prs = [], [], [], [], [], []
    for p in range(GDN_PAIRS):
        for h in (2 * p, 2 * p + 1):
            q = conv_silu(h * GDN_DK)
            k = conv_silu(GDN_QK + h * GDN_DK)
            vs.append(conv_silu(2 * GDN_QK + h * GDN_DV))
            qs.append(q * lax.rsqrt(jnp.sum(q * q, axis=-1, keepdims=True) + EPS) * (GDN_DK ** -0.5))
            ks.append(k * lax.rsqrt(jnp.sum(k * k, axis=-1, keepdims=True) + EPS))
            gcs.append(bcast(gcum, GA_LANE + _gdn_lane(h)))
        h0, h1 = 2 * p, 2 * p + 1
        gc_pr = jnp.where(left, gcs[h0], gcs[h1])
        bc_pr = jnp.where(left, bcast(beta_all, GB_LANE + _gdn_lane(h0)), bcast(beta_all, GB_LANE + _gdn_lane(h1)))
        dec_incl = jnp.exp(jnp.where(incl, gc_pr - row_g[p:p + 1, :], -jnp.inf))
        dec_strict = jnp.where(strict, dec_incl, 0.0)
        kst = jnp.concatenate([ks[h0], ks[h1]], axis=0).astype(BF)
        qst = jnp.concatenate([qs[h0], qs[h1]], axis=0).astype(BF)
        kk = _pair_diag_blocks(_dot_nt(kst, kst), left)
        a_prs.append(bc_pr * kk * dec_strict)
        qk_prs.append(_pair_diag_blocks(_dot_nt(qst, kst), left) * dec_incl)

    eye = (r == cc).astype(F32)
    ds = [eye - jnp.where((r >> 1) == (cc >> 1), a, 0.0) for a in a_prs]
    lvl = 1
    while (1 << lvl) < CHUNK:
        in_off = ((r >> (lvl + 1)) == (cc >> (lvl + 1))) & ((r >> lvl) != (cc >> lvl))
        for p in range(GDN_PAIRS):
            off = jnp.where(in_off, a_prs[p], 0.0)
            x = _dot_3pass(ds[p], _block_diag(off, left))
            ds[p] = ds[p] - _dot_3pass(x, _block_diag(ds[p], left))
        lvl += 1

    w_prs, ut_prs = [], []
    for p in range(GDN_PAIRS):
        h0, h1 = 2 * p, 2 * p + 1
        w_prs.append(_dot(ds[p] * row_beg[p:p + 1, :], _block_diag_wide(ks[h0], ks[h1])))
        ut_prs.append(_dot(ds[p] * row_b[p:p + 1, :], _block_diag_wide(vs[h0], vs[h1])))

    for p in range(GDN_PAIRS):
        us, qss, ss = [], [], []
        for j, h in enumerate((2 * p, 2 * p + 1)):
            s = sout_ref[0, h]
            wq = jnp.concatenate([w_prs[p][:, j * GDN_DK:(j + 1) * GDN_DK], qs[h]], axis=0)
            wqs = _dot(wq, s)
            us.append(ut_prs[p][:, j * GDN_DV:(j + 1) * GDN_DV] - wqs[:CHUNK])
            qss.append(wqs[CHUNK:])
            ss.append(s)
        qku = _dot(qk_prs[p], _block_diag_wide(us[0], us[1]))
        for j, h in enumerate((2 * p, 2 * p + 1)):
            gc = gcs[h]
            gl = gc[CHUNK - 1:CHUNK, :]
            o = jnp.exp(gc) * qss[j] + qku[:, j * GDN_DV:(j + 1) * GDN_DV]
            kd = ks[h] * jnp.exp(gl - gc)
            sout_ref[0, h] = jnp.exp(gl) * ss[j] + _dot_tn(kd, us[j])
            on = o * lax.rsqrt(jnp.mean(o * o, axis=-1, keepdims=True) + EPS) * normw_ref[...]
            zh = z_ref[:, h * GDN_DV:(h + 1) * GDN_DV]
            out_ref[:, h * GDN_DV:(h + 1) * GDN_DV] = (on * _silu(zh)).astype(out_ref.dtype)

    xbuf[8 - tail:8, :] = xbuf[8 + CHUNK - tail:8 + CHUNK, :]


def _gdn(proj, gates, conv0, s0, conv_w, alog_row, dtb_row, normw_row, nseq, nchunk):
    m = proj.shape[0]
    row = lambda b, ci: (b * nchunk + ci, 0)
    return pl.pallas_call(
        _gdn_kernel,
        grid=(nseq, nchunk),
        in_specs=[pl.BlockSpec((CHUNK, CONV_CH), row),
                  pl.BlockSpec((CHUNK, GDN_V), lambda b, ci: (b * nchunk + ci, CONV_CH // GDN_V)),
                  pl.BlockSpec((CHUNK, LANES), row),
                  pl.BlockSpec((1, CONV_W - 1, CONV_CH), lambda b, ci: (b, 0, 0)),
                  pl.BlockSpec((1, GDN_HEADS, GDN_DK, GDN_DV), lambda b, ci: (b, 0, 0, 0)),
                  pl.BlockSpec((CONV_W, CONV_CH), lambda b, ci: (0, 0)),
                  pl.BlockSpec((1, LANES), lambda b, ci: (0, 0)),
                  pl.BlockSpec((1, LANES), lambda b, ci: (0, 0)),
                  pl.BlockSpec((1, GDN_DV), lambda b, ci: (0, 0))],
        out_specs=[pl.BlockSpec((CHUNK, GDN_V), row),
                   pl.BlockSpec((1, GDN_HEADS, GDN_DK, GDN_DV), lambda b, ci: (b, 0, 0, 0)),
                   pl.BlockSpec((1, CONV_W - 1, CONV_CH), lambda b, ci: (b, 0, 0))],
        out_shape=[jax.ShapeDtypeStruct((m, GDN_V), BF),
                   jax.ShapeDtypeStruct((nseq, GDN_HEADS, GDN_DK, GDN_DV), F32),
                   jax.ShapeDtypeStruct((nseq, CONV_W - 1, CONV_CH), F32)],
        scratch_shapes=[pltpu.VMEM((8 + CHUNK, CONV_CH), F32)],
        compiler_params=pltpu.CompilerParams(dimension_semantics=("parallel", "arbitrary"),
                                             vmem_limit_bytes=VMEM_LIMIT),
        name="gdn",
    )(proj, proj, gates, conv0, s0, conv_w, alog_row, dtb_row, normw_row)


def _mlstm_kernel(qk_ref, v_ref, o_ref, z_ref, gate_ref, c0_ref, n0_ref, m0_ref, bi_ref, bf_ref, normw_ref,
                  out_ref, cout_ref, nout_ref, mout_ref):
    ci = pl.program_id(1)

    @pl.when(ci == 0)
    def _():
        cout_ref[...] = c0_ref[...]
        nout_ref[...] = n0_ref[...]
        mout_ref[...] = m0_ref[...]

    gate = gate_ref[...]
    logi_all = gate + bi_ref[...]
    logf_all = _log_sigmoid(gate + bf_ref[...])
    r = lax.broadcasted_iota(jnp.int32, (CHUNK, CHUNK), 0)
    c = lax.broadcasted_iota(jnp.int32, (CHUNK, CHUNK), 1)
    incl = r >= c
    bcum = _dot_hi(incl.astype(F32), logf_all)
    bcum_t = bcum.T
    logi_t = logi_all.T
    m_row = mout_ref[0]
    lane = lax.broadcasted_iota(jnp.int32, (1, LANES), 1)
    m_new_row = m_row

    for h in range(ML_HEADS):
        q = qk_ref[:, h * ML_DQK:(h + 1) * ML_DQK]
        k = qk_ref[:, ML_QK + h * ML_DQK:ML_QK + (h + 1) * ML_DQK] * (ML_DQK ** -0.5)
        v = v_ref[:, h * ML_DV:(h + 1) * ML_DV]
        b_c = bcum[:, MF_LANE + h:MF_LANE + h + 1]
        b_r = bcum_t[MF_LANE + h:MF_LANE + h + 1, :]
        li_c = logi_all[:, MI_LANE + h:MI_LANE + h + 1]
        li_r = logi_t[MI_LANE + h:MI_LANE + h + 1, :]
        m_prev = m_row[:, MF_LANE + h:MF_LANE + h + 1]
        dmat = jnp.where(incl, b_c - b_r + li_r, -jnp.inf)
        dmax = jnp.max(dmat, axis=-1, keepdims=True)
        qk = _dot_nt(q, k)
        inter = b_c + m_prev
        mt = jnp.maximum(inter, dmax)
        p = jnp.exp(dmat - mt) * qk
        wq = jnp.exp(inter - mt)
        cst = cout_ref[0, h]
        nst = nout_ref[0, h:h + 1, :]
        num = wq * _dot(q, cst) + _dot(p, v)
        qn = jnp.sum(q * nst, axis=-1, keepdims=True)
        den = wq * qn + jnp.sum(p, axis=-1, keepdims=True)
        hh = num / jnp.maximum(jnp.abs(den), jnp.exp(-mt))
        m_new = mt[CHUNK - 1:CHUNK, :]
        b_l = b_c[CHUNK - 1:CHUNK, :]
        wc = jnp.exp(b_l + m_prev - m_new)
        ws = jnp.exp(b_l - b_c + li_c - m_new)
        cout_ref[0, h] = wc * cst + _dot_tn(k, ws * v)
        ksum = jnp.sum(ws * k, axis=0, keepdims=True)
        nout_ref[0, h:h + 1, :] = wc * nst + ksum
        m_new_row = jnp.where(lane == MF_LANE + h, m_new, m_new_row)
        hn = hh * lax.rsqrt(jnp.mean(hh * hh, axis=-1, keepdims=True) + EPS) * normw_ref[...]
        og = o_ref[:, h * ML_DV:(h + 1) * ML_DV]
        zg = z_ref[:, h * ML_DV:(h + 1) * ML_DV]
        out_ref[:, h * ML_DV:(h + 1) * ML_DV] = (hn * _sigmoid(og) * _silu(zg)).astype(out_ref.dtype)

    mout_ref[0] = m_new_row


def _mlstm(proj, gates, c0, n0, m0, bi_row, bf_row, normw_row, nseq, nchunk):
    m = proj.shape[0]
    base = GDN_WIDE // ML_V
    blk = lambda j: (lambda b, ci: (b * nchunk + ci, base + j))
    return pl.pallas_call(
        _mlstm_kernel,
        grid=(nseq, nchunk),
        in_specs=[pl.BlockSpec((CHUNK, 2 * ML_QK), blk(0)),
                  pl.BlockSpec((CHUNK, ML_V), blk(1)),
                  pl.BlockSpec((CHUNK, ML_V), blk(2)),
                  pl.BlockSpec((CHUNK, ML_V), blk(3)),
                  pl.BlockSpec((CHUNK, LANES), lambda b, ci: (b * nchunk + ci, 0)),
                  pl.BlockSpec((1, ML_HEADS, ML_DQK, ML_DV), lambda b, ci: (b, 0, 0, 0)),
                  pl.BlockSpec((1, ML_HEADS, ML_DQK), lambda b, ci: (b, 0, 0)),
                  pl.BlockSpec((1, 1, LANES), lambda b, ci: (b, 0, 0)),
                  pl.BlockSpec((1, LANES), lambda b, ci: (0, 0)),
                  pl.BlockSpec((1, LANES), lambda b, ci: (0, 0)),
                  pl.BlockSpec((1, ML_DV), lambda b, ci: (0, 0))],
        out_specs=[pl.BlockSpec((CHUNK, ML_V), lambda b, ci: (b * nchunk + ci, 0)),
                   pl.BlockSpec((1, ML_HEADS, ML_DQK, ML_DV), lambda b, ci: (b, 0, 0, 0)),
                   pl.BlockSpec((1, ML_HEADS, ML_DQK), lambda b, ci: (b, 0, 0)),
                   pl.BlockSpec((1, 1, LANES), lambda b, ci: (b, 0, 0))],
        out_shape=[jax.ShapeDtypeStruct((m, ML_V), BF),
                   jax.ShapeDtypeStruct((nseq, ML_HEADS, ML_DQK, ML_DV), F32),
                   jax.ShapeDtypeStruct((nseq, ML_HEADS, ML_DQK), F32),
                   jax.ShapeDtypeStruct((nseq, 1, LANES), F32)],
        compiler_params=pltpu.CompilerParams(dimension_semantics=("parallel", "arbitrary"),
                                             vmem_limit_bytes=VMEM_LIMIT),
        name="mlstm",
    )(proj, proj, proj, proj, gates, c0, n0, m0, bi_row, bf_row, normw_row)


OUT_NORM_ROWS = 64


def _out_kernel(a_ref, b_ref, wa_ref, wb_ref, x_ref, g_ref, y_ref):
    j = pl.program_id(1)
    tn = x_ref.shape[1]
    cols = pl.ds(pl.multiple_of(j * tn, tn), tn)
    y_ref[:, cols] = (x_ref[...] + jnp.dot(a_ref[...], wa_ref[...], preferred_element_type=F32)
                      + jnp.dot(b_ref[...], wb_ref[...], preferred_element_type=F32))

    @pl.when(j == pl.num_programs(1) - 1)
    def _():
        def body(rb, carry):
            rows = pl.ds(pl.multiple_of(rb * OUT_NORM_ROWS, OUT_NORM_ROWS), OUT_NORM_ROWS)
            y = y_ref[rows, :]
            y_ref[rows, :] = y * lax.rsqrt(jnp.mean(y * y, axis=-1, keepdims=True) + EPS) * g_ref[...]
            return carry

        lax.fori_loop(0, y_ref.shape[0] // OUT_NORM_ROWS, body, 0)


def _outproj(gdn_o, ml_o, w_a, w_b, x2d, final_g, tm=1024, tn=256):
    m = x2d.shape[0]
    return pl.pallas_call(
        _out_kernel,
        grid=(m // tm, D_MODEL // tn),
        in_specs=[pl.BlockSpec((tm, GDN_V), lambda i, j: (i, 0)),
                  pl.BlockSpec((tm, ML_V), lambda i, j: (i, 0)),
                  pl.BlockSpec((GDN_V, tn), lambda i, j: (0, j)),
                  pl.BlockSpec((ML_V, tn), lambda i, j: (0, j)),
                  pl.BlockSpec((tm, tn), lambda i, j: (i, j)),
                  pl.BlockSpec((1, D_MODEL), lambda i, j: (0, 0))],
        out_specs=pl.BlockSpec((tm, D_MODEL), lambda i, j: (i, 0)),
        out_shape=jax.ShapeDtypeStruct((m, D_MODEL), F32),
        compiler_params=pltpu.CompilerParams(dimension_semantics=("parallel", "arbitrary"),
                                             vmem_limit_bytes=OUT_VMEM_LIMIT),
        name="out_proj",
    )(gdn_o, ml_o, w_a, w_b, x2d, final_g)


def _lane_row(vec, lane0):
    return jnp.zeros((1, LANES), F32).at[0, lane0:lane0 + vec.shape[0]].set(vec.astype(F32))


def _layer(x, state, wts):
    (w_wide, w_gate, norm_g, conv_w, alog_row, dtb_row, gdn_normw, bi_row, bf_row, ml_normw,
     w_out_a, w_out_b, final_g) = wts
    conv0, s0, c0, n0, m0 = state
    nseq, t, _ = x.shape
    nchunk = t // CHUNK
    x2d = x.reshape(nseq * t, D_MODEL)
    h, gates = _prep(x2d, norm_g, w_gate)
    proj = _inproj(h, w_wide)
    gdn_o, s_new, conv_new = _gdn(proj, gates, conv0, s0, conv_w, alog_row, dtb_row, gdn_normw, nseq, nchunk)
    m0_row = jnp.zeros((nseq, 1, LANES), F32).at[:, 0, MF_LANE:MF_LANE + ML_HEADS].set(m0)
    ml_o, c_new, n_new, m_new_row = _mlstm(proj, gates, c0, n0, m0_row, bi_row, bf_row, ml_normw, nseq, nchunk)
    y = _outproj(gdn_o, ml_o, w_out_a, w_out_b, x2d, final_g)
    m_new = m_new_row[:, 0, MF_LANE:MF_LANE + ML_HEADS]
    return y.reshape(nseq, t, D_MODEL), (conv_new, s_new, c_new, n_new, m_new)


def kernel(x_prompt, x_sample, cache_gdn_conv, state_gdn, state_mlstm_c, state_mlstm_n, state_mlstm_m,
           norm_g, w_in, conv_w, gdn_a_log, gdn_dt_bias, gdn_norm_w, mlstm_b_i, mlstm_b_f, mlstm_norm_w,
           w_out, final_g):
    assert w_in.shape[0] == 1, "single-layer model"
    w = w_in[0]
    g_mid = GDN_WIDE + GDN_HEADS
    g_end = GDN_WIDE + 2 * GDN_HEADS
    m_end = g_end + ML_WIDE
    w_wide = jnp.concatenate([w[:, :GDN_WIDE], w[:, g_end:m_end]], axis=1).astype(BF)
    w_gate = jnp.concatenate(
        [w[:, GDN_WIDE:g_mid][:, GDN_HEAD_ORDER], w[:, g_mid:g_end][:, GDN_HEAD_ORDER], w[:, m_end:],
         jnp.zeros((D_MODEL, LANES - 2 * GDN_HEADS - 2 * ML_HEADS), F32)], axis=1).astype(BF)
    wts = (w_wide, w_gate, norm_g[0][None, :], conv_w[0],
           _lane_row(gdn_a_log[0][GDN_HEAD_ORDER], GA_LANE), _lane_row(gdn_dt_bias[0][GDN_HEAD_ORDER], GA_LANE),
           gdn_norm_w[0][None, :],
           _lane_row(mlstm_b_i[0], MI_LANE), _lane_row(mlstm_b_f[0], MF_LANE), mlstm_norm_w[0][None, :],
           w_out[0, :GDN_V].astype(BF), w_out[0, GDN_V:].astype(BF), final_g[None, :])
    nb = x_prompt.shape[0]
    fresh = (jnp.zeros((nb, CONV_W - 1, CONV_CH), F32),
             jnp.zeros((nb, GDN_HEADS, GDN_DK, GDN_DV), F32),
             jnp.zeros((nb, ML_HEADS, ML_DQK, ML_DV), F32),
             jnp.zeros((nb, ML_HEADS, ML_DQK), F32),
             jnp.zeros((nb, ML_HEADS), F32))
    carried = (cache_gdn_conv[0], state_gdn[0], state_mlstm_c[0], state_mlstm_n[0], state_mlstm_m[0])
    yp, sp = _layer(x_prompt, fresh, wts)
    ys, ss = _layer(x_sample, carried, wts)
    return (yp, ys) + tuple(a[None] for a in sp) + tuple(a[None] for a in ss)
```

```python
import jax
import jax.numpy as jnp
import numpy as np
from jax import lax
from jax.experimental import pallas as pl
from jax.experimental.pallas import tpu as pltpu

D_MODEL = 4096
CHUNK = 64
GDN_DK = 128
GDN_DV = 128
GDN_HEADS = 16
GDN_PAIRS = GDN_HEADS // 2
GDN_QK = GDN_HEADS * GDN_DK
GDN_V = GDN_HEADS * GDN_DV
ML_HEADS = 8
ML_DV = 256
ML_DQK = 128
ML_QK = ML_HEADS * ML_DQK
ML_V = ML_HEADS * ML_DV
CONV_W = 4
CONV_CH = 2 * GDN_QK + GDN_V
EPS = 1e-6

GDN_WIDE = 2 * GDN_QK + 2 * GDN_V
ML_WIDE = 2 * ML_QK + 3 * ML_V
WIDE = GDN_WIDE + ML_WIDE
LANES = 128
GB_LANE = 0
GA_LANE = GDN_HEADS
MI_LANE = 2 * GDN_HEADS
MF_LANE = 2 * GDN_HEADS + ML_HEADS
GDN_HEAD_ORDER = np.concatenate([np.arange(0, GDN_HEADS, 2), np.arange(1, GDN_HEADS, 2)])

VMEM_LIMIT = 48 * 1024 * 1024
OUT_VMEM_LIMIT = 60 * 1024 * 1024
HI = lax.Precision.HIGHEST
BF = jnp.bfloat16
F32 = jnp.float32


def _gdn_lane(h):
    return (h % 2) * GDN_PAIRS + h // 2


def _dot(a, b):
    return jnp.dot(a.astype(BF), b.astype(BF), preferred_element_type=F32)


def _dot_nt(a, b):
    return lax.dot_general(a.astype(BF), b.astype(BF), (((1,), (1,)), ((), ())), preferred_element_type=F32)


def _dot_tn(a, b):
    return lax.dot_general(a.astype(BF), b.astype(BF), (((0,), (0,)), ((), ())), preferred_element_type=F32)


def _dot_hi(a, b):
    return jnp.dot(a, b, precision=HI, preferred_element_type=F32)


def _split(x):
    hi = x.astype(BF)
    lo = (x - hi.astype(F32)).astype(BF)
    return hi, lo


def _dot_3pass(a, b):
    a_hi, a_lo = _split(a)
    b_hi, b_lo = _split(b)
    return jnp.dot(jnp.concatenate([a_hi, a_lo, a_hi], axis=1), jnp.concatenate([b_hi, b_hi, b_lo], axis=0),
                   preferred_element_type=F32)


def _sigmoid(x):
    return 1.0 / (1.0 + jnp.exp(-x))


def _silu(x):
    return x * _sigmoid(x)


def _softplus(x):
    return jnp.maximum(x, 0.0) + jnp.log(1.0 + jnp.exp(-jnp.abs(x)))


def _log_sigmoid(x):
    return -_softplus(-x)


def _prep_kernel(x_ref, g_ref, wg_ref, h_ref, gate_ref):
    x = x_ref[...]
    h = x * lax.rsqrt(jnp.mean(x * x, axis=-1, keepdims=True) + EPS) * g_ref[...]
    hb = h.astype(BF)
    h_ref[...] = hb
    gate_ref[...] = jnp.dot(hb, wg_ref[...], preferred_element_type=F32)


def _prep(x2d, norm_g, w_gate, tm=512):
    m = x2d.shape[0]
    return pl.pallas_call(
        _prep_kernel,
        grid=(m // tm,),
        in_specs=[pl.BlockSpec((tm, D_MODEL), lambda i: (i, 0)),
                  pl.BlockSpec((1, D_MODEL), lambda i: (0, 0)),
                  pl.BlockSpec((D_MODEL, LANES), lambda i: (0, 0))],
        out_specs=[pl.BlockSpec((tm, D_MODEL), lambda i: (i, 0)),
                   pl.BlockSpec((tm, LANES), lambda i: (i, 0))],
        out_shape=[jax.ShapeDtypeStruct((m, D_MODEL), BF),
                   jax.ShapeDtypeStruct((m, LANES), F32)],
        compiler_params=pltpu.CompilerParams(dimension_semantics=("parallel",), vmem_limit_bytes=VMEM_LIMIT),
        name="prep",
    )(x2d, norm_g, w_gate)


def _inproj_kernel(h_ref, w_ref, o_ref):
    o_ref[...] = jnp.dot(h_ref[...], w_ref[...], preferred_element_type=F32)


def _inproj(h, w, n, tm=1024, tn=512):
    m = h.shape[0]
    return pl.pallas_call(
        _inproj_kernel,
        grid=(m // tm, n // tn),
        in_specs=[pl.BlockSpec((tm, D_MODEL), lambda i, j: (i, 0)),
                  pl.BlockSpec((D_MODEL, tn), lambda i, j: (0, j))],
        out_specs=pl.BlockSpec((tm, tn), lambda i, j: (i, j)),
        out_shape=jax.ShapeDtypeStruct((m, n), F32),
        compiler_params=pltpu.CompilerParams(dimension_semantics=("parallel", "arbitrary"),
                                             vmem_limit_bytes=VMEM_LIMIT),
        name="in_proj",
    )(h, w)


def _block_diag(x, left):
    return jnp.concatenate([jnp.where(left, x, 0.0), jnp.where(left, 0.0, x)], axis=0)


def _block_diag_wide(x0, x1):
    z = jnp.zeros_like(x0)
    return jnp.concatenate([jnp.concatenate([x0, z], axis=1), jnp.concatenate([z, x1], axis=1)], axis=0)


def _pair_diag_blocks(m, left):
    return jnp.where(left, m[:CHUNK], m[CHUNK:])


def _gdn_kernel(qkv_ref, z_ref, gate_ref, conv0_ref, s0_ref, convw_ref, alog_ref, dtb_ref, normw_ref,
                out_ref, sout_ref, convout_ref, xbuf):
    ci = pl.program_id(1)
    tail = CONV_W - 1

    @pl.when(ci == 0)
    def _():
        xbuf[8 - tail:8, :] = conv0_ref[0]
        sout_ref[...] = s0_ref[...]

    xbuf[8:8 + CHUNK, :] = qkv_ref[...]
    convout_ref[0] = qkv_ref[CHUNK - tail:CHUNK, :]

    r = lax.broadcasted_iota(jnp.int32, (CHUNK, LANES), 0)
    c = lax.broadcasted_iota(jnp.int32, (CHUNK, LANES), 1)
    left = c < CHUNK
    cc = c & (CHUNK - 1)
    incl = r >= cc
    strict = r > cc

    gate = gate_ref[...]
    beta_all = _sigmoid(gate)
    g_all = -jnp.exp(alog_ref[...]) * _softplus(gate + dtb_ref[...])
    rr = lax.broadcasted_iota(jnp.int32, (CHUNK, CHUNK), 0)
    rc = lax.broadcasted_iota(jnp.int32, (CHUNK, CHUNK), 1)
    gcum = _dot_hi((rr >= rc).astype(F32), g_all)
    comb_t = jnp.where(c < GA_LANE, beta_all, gcum).T
    row_b = jnp.concatenate([comb_t[GB_LANE:GB_LANE + GDN_PAIRS], comb_t[GB_LANE + GDN_PAIRS:GA_LANE]], axis=1)
    row_g = jnp.concatenate([comb_t[GA_LANE:GA_LANE + GDN_PAIRS], comb_t[GA_LANE + GDN_PAIRS:MI_LANE]], axis=1)
    row_beg = row_b * jnp.exp(row_g)

    def conv_silu(col):
        y = xbuf[5:5 + CHUNK, col:col + LANES] * convw_ref[0:1, col:col + LANES]
        for j in range(1, CONV_W):
            y = y + xbuf[5 + j:5 + j + CHUNK, col:col + LANES] * convw_ref[j:j + 1, col:col + LANES]
        return _silu(y)

    def bcast(x, lane):
        return jnp.broadcast_to(x[:, lane:lane + 1], (CHUNK, LANES))

    pairs = range(GDN_PAIRS)
    heads = range(GDN_HEADS)

    qs, ks, vs, gcs, dec_incls, bc_prs, ksts, qsts = [], [], [], [], [], [], [], []
    for p in pairs:
        for h in (2 * p, 2 * p + 1):
            q = conv_silu(h * GDN_DK)
            k = conv_silu(GDN_QK + h * GDN_DK)
            vs.append(conv_silu(2 * GDN_QK + h * GDN_DV))
            qs.append(q * lax.rsqrt(jnp.sum(q * q, axis=-1, keepdims=True) + EPS) * (GDN_DK ** -0.5))
            ks.append(k * lax.rsqrt(jnp.sum(k * k, axis=-1, keepdims=True) + EPS))
            gcs.append(bcast(gcum, GA_LANE + _gdn_lane(h)))
        h0, h1 = 2 * p, 2 * p + 1
        gc_pr = jnp.where(left, gcs[h0], gcs[h1])
        bc_prs.append(jnp.where(left, bcast(beta_all, GB_LANE + _gdn_lane(h0)),
                                bcast(beta_all, GB_LANE + _gdn_lane(h1))))
        dec_incls.append(jnp.exp(jnp.where(incl, gc_pr - row_g[p:p + 1, :], -jnp.inf)))
        ksts.append(jnp.concatenate([ks[h0], ks[h1]], axis=0).astype(BF))
        qsts.append(jnp.concatenate([qs[h0], qs[h1]], axis=0).astype(BF))

    kks = [_dot_nt(ksts[p], ksts[p]) for p in pairs]
    qks = [_dot_nt(qsts[p], ksts[p]) for p in pairs]
    a_prs = [bc_prs[p] * _pair_diag_blocks(kks[p], left) * jnp.where(strict, dec_incls[p], 0.0) for p in pairs]
    qk_prs = [_pair_diag_blocks(qks[p], left) * dec_incls[p] for p in pairs]

    eye = (r == cc).astype(F32)
    ds = [eye - jnp.where((r >> 1) == (cc >> 1), a, 0.0) for a in a_prs]
    lvl = 1
    while (1 << lvl) < CHUNK:
        in_off = ((r >> (lvl + 1)) == (cc >> (lvl + 1))) & ((r >> lvl) != (cc >> lvl))
        xs = [_dot_3pass(ds[p], _block_diag(jnp.where(in_off, a_prs[p], 0.0), left)) for p in pairs]
        ds = [ds[p] - _dot_3pass(xs[p], _block_diag(ds[p], left)) for p in pairs]
        lvl += 1

    w_prs = [_dot(ds[p] * row_beg[p:p + 1, :], _block_diag_wide(ks[2 * p], ks[2 * p + 1])) for p in pairs]
    ut_prs = [_dot(ds[p] * row_b[p:p + 1, :], _block_diag_wide(vs[2 * p], vs[2 * p + 1])) for p in pairs]

    def half(x, h):
        return x[h // 2][:, (h % 2) * LANES:(h % 2 + 1) * LANES]

    ss = [sout_ref[0, h] for h in heads]
    wqs = [_dot(jnp.concatenate([half(w_prs, h), qs[h]], axis=0), ss[h]) for h in heads]
    us = [half(ut_prs, h) - wqs[h][:CHUNK] for h in heads]
    qkus = [_dot(qk_prs[p], _block_diag_wide(us[2 * p], us[2 * p + 1])) for p in pairs]
    gls = [gcs[h][CHUNK - 1:CHUNK, :] for h in heads]
    kds = [ks[h] * jnp.exp(gls[h] - gcs[h]) for h in heads]
    kdus = [_dot_tn(kds[h], us[h]) for h in heads]
    for h in heads:
        sout_ref[0, h] = jnp.exp(gls[h]) * ss[h] + kdus[h]

    for h in heads:
        o = jnp.exp(gcs[h]) * wqs[h][CHUNK:] + half(qkus, h)
        on = o * lax.rsqrt(jnp.mean(o * o, axis=-1, keepdims=True) + EPS) * normw_ref[...]
        zh = z_ref[:, h * GDN_DV:(h + 1) * GDN_DV]
        out_ref[:, h * GDN_DV:(h + 1) * GDN_DV] = (on * _silu(zh)).astype(out_ref.dtype)

    xbuf[8 - tail:8, :] = xbuf[8 + CHUNK - tail:8 + CHUNK, :]


def _gdn(proj, gates, conv0, s0, conv_w, alog_row, dtb_row, normw_row, nseq, nchunk):
    m = proj.shape[0]
    row = lambda b, ci: (b * nchunk + ci, 0)
    return pl.pallas_call(
        _gdn_kernel,
        grid=(nseq, nchunk),
        in_specs=[pl.BlockSpec((CHUNK, CONV_CH), row),
                  pl.BlockSpec((CHUNK, GDN_V), lambda b, ci: (b * nchunk + ci, CONV_CH // GDN_V)),
                  pl.BlockSpec((CHUNK, LANES), row),
                  pl.BlockSpec((1, CONV_W - 1, CONV_CH), lambda b, ci: (b, 0, 0)),
                  pl.BlockSpec((1, GDN_HEADS, GDN_DK, GDN_DV), lambda b, ci: (b, 0, 0, 0)),
                  pl.BlockSpec((CONV_W, CONV_CH), lambda b, ci: (0, 0)),
                  pl.BlockSpec((1, LANES), lambda b, ci: (0, 0)),
                  pl.BlockSpec((1, LANES), lambda b, ci: (0, 0)),
                  pl.BlockSpec((1, GDN_DV), lambda b, ci: (0, 0))],
        out_specs=[pl.BlockSpec((CHUNK, GDN_V), row),
                   pl.BlockSpec((1, GDN_HEADS, GDN_DK, GDN_DV), lambda b, ci: (b, 0, 0, 0)),
                   pl.BlockSpec((1, CONV_W - 1, CONV_CH), lambda b, ci: (b, 0, 0))],
        out_shape=[jax.ShapeDtypeStruct((m, GDN_V), BF),
                   jax.ShapeDtypeStruct((nseq, GDN_HEADS, GDN_DK, GDN_DV), F32),
                   jax.ShapeDtypeStruct((nseq, CONV_W - 1, CONV_CH), F32)],
        scratch_shapes=[pltpu.VMEM((8 + CHUNK, CONV_CH), F32)],
        compiler_params=pltpu.CompilerParams(dimension_semantics=("parallel", "arbitrary"),
                                             vmem_limit_bytes=VMEM_LIMIT),
        name="gdn",
    )(proj, proj, gates, conv0, s0, conv_w, alog_row, dtb_row, normw_row)


def _mlstm_kernel(qk_ref, v_ref, o_ref, z_ref, gate_ref, c0_ref, n0_ref, m0_ref, bi_ref, bf_ref, normw_ref,
                  out_ref, cout_ref, nout_ref, mout_ref):
    ci = pl.program_id(1)

    @pl.when(ci == 0)
    def _():
        cout_ref[...] = c0_ref[...]
        nout_ref[...] = n0_ref[...]
        mout_ref[...] = m0_ref[...]

    gate = gate_ref[...]
    logi_all = gate + bi_ref[...]
    logf_all = _log_sigmoid(gate + bf_ref[...])
    r = lax.broadcasted_iota(jnp.int32, (CHUNK, CHUNK), 0)
    c = lax.broadcasted_iota(jnp.int32, (CHUNK, CHUNK), 1)
    incl = r >= c
    bcum = _dot_hi(incl.astype(F32), logf_all)
    bcum_t = bcum.T
    logi_t = logi_all.T
    m_row = mout_ref[0]
    lane = lax.broadcasted_iota(jnp.int32, (1, LANES), 1)
    m_new_row = m_row

    heads = range(ML_HEADS)
    qs = [qk_ref[:, h * ML_DQK:(h + 1) * ML_DQK] for h in heads]
    ks = [qk_ref[:, ML_QK + h * ML_DQK:ML_QK + (h + 1) * ML_DQK] * (ML_DQK ** -0.5) for h in heads]
    vs = [v_ref[:, h * ML_DV:(h + 1) * ML_DV] for h in heads]
    csts = [cout_ref[0, h] for h in heads]
    nsts = [nout_ref[0, h:h + 1, :] for h in heads]
    qks = [_dot_nt(qs[h], ks[h]) for h in heads]
    qcs = [_dot(qs[h], csts[h]) for h in heads]

    ps, wqs, mts, wss, wcs = [], [], [], [], []
    for h in heads:
        b_c = bcum[:, MF_LANE + h:MF_LANE + h + 1]
        b_r = bcum_t[MF_LANE + h:MF_LANE + h + 1, :]
        li_c = logi_all[:, MI_LANE + h:MI_LANE + h + 1]
        li_r = logi_t[MI_LANE + h:MI_LANE + h + 1, :]
        m_prev = m_row[:, MF_LANE + h:MF_LANE + h + 1]
        dmat = jnp.where(incl, b_c - b_r + li_r, -jnp.inf)
        dmax = jnp.max(dmat, axis=-1, keepdims=True)
        inter = b_c + m_prev
        mt = jnp.maximum(inter, dmax)
        ps.append(jnp.exp(dmat - mt) * qks[h])
        wqs.append(jnp.exp(inter - mt))
        mts.append(mt)
        m_new = mt[CHUNK - 1:CHUNK, :]
        b_l = b_c[CHUNK - 1:CHUNK, :]
        wcs.append(jnp.exp(b_l + m_prev - m_new))
        wss.append(jnp.exp(b_l - b_c + li_c - m_new))
        m_new_row = jnp.where(lane == MF_LANE + h, m_new, m_new_row)

    pvs = [_dot(ps[h], vs[h]) for h in heads]
    kvs = [_dot_tn(ks[h], wss[h] * vs[h]) for h in heads]
    for h in heads:
        cout_ref[0, h] = wcs[h] * csts[h] + kvs[h]
        nout_ref[0, h:h + 1, :] = wcs[h] * nsts[h] + jnp.sum(wss[h] * ks[h], axis=0, keepdims=True)
    mout_ref[0] = m_new_row

    for h in heads:
        num = wqs[h] * qcs[h] + pvs[h]
        qn = jnp.sum(qs[h] * nsts[h], axis=-1, keepdims=True)
        den = wqs[h] * qn + jnp.sum(ps[h], axis=-1, keepdims=True)
        hh = num / jnp.maximum(jnp.abs(den), jnp.exp(-mts[h]))
        hn = hh * lax.rsqrt(jnp.mean(hh * hh, axis=-1, keepdims=True) + EPS) * normw_ref[...]
        og = o_ref[:, h * ML_DV:(h + 1) * ML_DV]
        zg = z_ref[:, h * ML_DV:(h + 1) * ML_DV]
        out_ref[:, h * ML_DV:(h + 1) * ML_DV] = (hn * _sigmoid(og) * _silu(zg)).astype(out_ref.dtype)


def _mlstm(proj, gates, c0, n0, m0, bi_row, bf_row, normw_row, nseq, nchunk):
    m = proj.shape[0]
    blk = lambda j: (lambda b, ci: (b * nchunk + ci, j))
    return pl.pallas_call(
        _mlstm_kernel,
        grid=(nseq, nchunk),
        in_specs=[pl.BlockSpec((CHUNK, 2 * ML_QK), blk(0)),
                  pl.BlockSpec((CHUNK, ML_V), blk(1)),
                  pl.BlockSpec((CHUNK, ML_V), blk(2)),
                  pl.BlockSpec((CHUNK, ML_V), blk(3)),
                  pl.BlockSpec((CHUNK, LANES), lambda b, ci: (b * nchunk + ci, 0)),
                  pl.BlockSpec((1, ML_HEADS, ML_DQK, ML_DV), lambda b, ci: (b, 0, 0, 0)),
                  pl.BlockSpec((1, ML_HEADS, ML_DQK), lambda b, ci: (b, 0, 0)),
                  pl.BlockSpec((1, 1, LANES), lambda b, ci: (b, 0, 0)),
                  pl.BlockSpec((1, LANES), lambda b, ci: (0, 0)),
                  pl.BlockSpec((1, LANES), lambda b, ci: (0, 0)),
                  pl.BlockSpec((1, ML_DV), lambda b, ci: (0, 0))],
        out_specs=[pl.BlockSpec((CHUNK, ML_V), lambda b, ci: (b * nchunk + ci, 0)),
                   pl.BlockSpec((1, ML_HEADS, ML_DQK, ML_DV), lambda b, ci: (b, 0, 0, 0)),
                   pl.BlockSpec((1, ML_HEADS, ML_DQK), lambda b, ci: (b, 0, 0)),
                   pl.BlockSpec((1, 1, LANES), lambda b, ci: (b, 0, 0))],
        out_shape=[jax.ShapeDtypeStruct((m, ML_V), BF),
                   jax.ShapeDtypeStruct((nseq, ML_HEADS, ML_DQK, ML_DV), F32),
                   jax.ShapeDtypeStruct((nseq, ML_HEADS, ML_DQK), F32),
                   jax.ShapeDtypeStruct((nseq, 1, LANES), F32)],
        compiler_params=pltpu.CompilerParams(dimension_semantics=("parallel", "arbitrary"),
                                             vmem_limit_bytes=VMEM_LIMIT),
        name="mlstm",
    )(proj, proj, proj, proj, gates, c0, n0, m0, bi_row, bf_row, normw_row)


OUT_NORM_ROWS = 64


def _out_kernel(a_ref, b_ref, wa_ref, wb_ref, x_ref, g_ref, y_ref):
    j = pl.program_id(1)
    tn = x_ref.shape[1]
    cols = pl.ds(pl.multiple_of(j * tn, tn), tn)
    y_ref[:, cols] = (x_ref[...] + jnp.dot(a_ref[...], wa_ref[...], preferred_element_type=F32)
                      + jnp.dot(b_ref[...], wb_ref[...], preferred_element_type=F32))

    @pl.when(j == pl.num_programs(1) - 1)
    def _():
        def body(rb, carry):
            rows = pl.ds(pl.multiple_of(rb * OUT_NORM_ROWS, OUT_NORM_ROWS), OUT_NORM_ROWS)
            y = y_ref[rows, :]
            y_ref[rows, :] = y * lax.rsqrt(jnp.mean(y * y, axis=-1, keepdims=True) + EPS) * g_ref[...]
            return carry

        lax.fori_loop(0, y_ref.shape[0] // OUT_NORM_ROWS, body, 0)


def _outproj(gdn_o, ml_o, w_out_bf, x2d, final_g, tm=1024, tn=256):
    m = x2d.shape[0]
    assert GDN_V == ML_V
    return pl.pallas_call(
        _out_kernel,
        grid=(m // tm, D_MODEL // tn),
        in_specs=[pl.BlockSpec((tm, GDN_V), lambda i, j: (i, 0)),
                  pl.BlockSpec((tm, ML_V), lambda i, j: (i, 0)),
                  pl.BlockSpec((GDN_V, tn), lambda i, j: (0, j)),
                  pl.BlockSpec((ML_V, tn), lambda i, j: (1, j)),
                  pl.BlockSpec((tm, tn), lambda i, j: (i, j)),
                  pl.BlockSpec((1, D_MODEL), lambda i, j: (0, 0))],
        out_specs=pl.BlockSpec((tm, D_MODEL), lambda i, j: (i, 0)),
        out_shape=jax.ShapeDtypeStruct((m, D_MODEL), F32),
        compiler_params=pltpu.CompilerParams(dimension_semantics=("parallel", "arbitrary"),
                                             vmem_limit_bytes=OUT_VMEM_LIMIT),
        name="out_proj",
    )(gdn_o, ml_o, w_out_bf, w_out_bf, x2d, final_g)


def _lane_row(vec, lane0):
    return jnp.zeros((1, LANES), F32).at[0, lane0:lane0 + vec.shape[0]].set(vec.astype(F32))


def _layer(x, state, wts):
    (w_bf, w_ml, w_gate, norm_g, conv_w, alog_row, dtb_row, gdn_normw, bi_row, bf_row, ml_normw,
     w_out_bf, final_g) = wts
    conv0, s0, c0, n0, m0 = state
    nseq, t, _ = x.shape
    nchunk = t // CHUNK
    x2d = x.reshape(nseq * t, D_MODEL)
    h, gates = _prep(x2d, norm_g, w_gate)
    proj_g = _inproj(h, w_bf, GDN_WIDE)
    proj_m = _inproj(h, w_ml, ML_WIDE)
    gdn_o, s_new, conv_new = _gdn(proj_g, gates, conv0, s0, conv_w, alog_row, dtb_row, gdn_normw, nseq, nchunk)
    m0_row = jnp.zeros((nseq, 1, LANES), F32).at[:, 0, MF_LANE:MF_LANE + ML_HEADS].set(m0)
    ml_o, c_new, n_new, m_new_row = _mlstm(proj_m, gates, c0, n0, m0_row, bi_row, bf_row, ml_normw, nseq, nchunk)
    y = _outproj(gdn_o, ml_o, w_out_bf, x2d, final_g)
    m_new = m_new_row[:, 0, MF_LANE:MF_LANE + ML_HEADS]
    return y.reshape(nseq, t, D_MODEL), (conv_new, s_new, c_new, n_new, m_new)


def kernel(x_prompt, x_sample, cache_gdn_conv, state_gdn, state_mlstm_c, state_mlstm_n, state_mlstm_m,
           norm_g, w_in, conv_w, gdn_a_log, gdn_dt_bias, gdn_norm_w, mlstm_b_i, mlstm_b_f, mlstm_norm_w,
           w_out, final_g):
    assert w_in.shape[0] == 1, "single-layer model"
    w_bf = w_in[0].astype(BF)
    g_mid = GDN_WIDE + GDN_HEADS
    g_end = GDN_WIDE + 2 * GDN_HEADS
    m_end = g_end + ML_WIDE
    w_ml = w_bf[:, g_end:m_end]
    w_gate = jnp.concatenate(
        [w_bf[:, GDN_WIDE:g_mid][:, GDN_HEAD_ORDER], w_bf[:, g_mid:g_end][:, GDN_HEAD_ORDER], w_bf[:, m_end:],
         jnp.zeros((D_MODEL, LANES - 2 * GDN_HEADS - 2 * ML_HEADS), BF)], axis=1)
    wts = (w_bf, w_ml, w_gate, norm_g[0][None, :], conv_w[0],
           _lane_row(gdn_a_log[0][GDN_HEAD_ORDER], GA_LANE), _lane_row(gdn_dt_bias[0][GDN_HEAD_ORDER], GA_LANE),
           gdn_norm_w[0][None, :],
           _lane_row(mlstm_b_i[0], MI_LANE), _lane_row(mlstm_b_f[0], MF_LANE), mlstm_norm_w[0][None, :],
           w_out[0].astype(BF), final_g[None, :])
    nb = x_prompt.shape[0]
    fresh = (jnp.zeros((nb, CONV_W - 1, CONV_CH), F32),
             jnp.zeros((nb, GDN_HEADS, GDN_DK, GDN_DV), F32),
             jnp.zeros((nb, ML_HEADS, ML_DQK, ML_DV), F32),
             jnp.zeros((nb, ML_HEADS, ML_DQK), F32),
             jnp.zeros((nb, ML_HEADS), F32))
    carried = (cache_gdn_conv[0], state_gdn[0], state_mlstm_c[0], state_mlstm_n[0], state_mlstm_m[0])
    yp, sp = _layer(x_prompt, fresh, wts)
    ys, ss = _layer(x_sample, carried, wts)
    return (yp, ys) + tuple(a[None] for a in sp) + tuple(a[None] for a in ss)
```
